```python
import math
import jax, jax.numpy as jnp
from jax import lax
import numpy as np

D_MODEL = 2048
BATCH = 8
SEQ = 2048
DEPTH = 2

GRID_W = 64
CTX_LEN = 256
MIX_WIDTH = D_MODEL
A_HEAD_DIM = 128
A_WIDTH = D_MODEL // 2
A_HEADS = A_WIDTH // A_HEAD_DIM
B_HEADS = 4
B_VAL_WIDTH = D_MODEL // 2
B_VAL_DIM = B_VAL_WIDTH // B_HEADS
B_KEY_WIDTH = B_VAL_WIDTH // 2
B_KEY_DIM = B_KEY_WIDTH // B_HEADS
GATE_RANK = 16
GLA_GATE_TEMP = 16.0
C_HEAD_DIM = 128
C_HEADS = D_MODEL // (2 * C_HEAD_DIM)
C_VAL_DIM = 2 * C_HEAD_DIM
Q_BLOCK = 128
CHUNK = 32
ROPE_BASE = 10000.0
EPS = 1e-6
N_EVEN = (DEPTH + 1) // 2
N_ODD = DEPTH // 2
EVEN_IN_WIDTH = 4 * A_WIDTH + 2 * B_KEY_WIDTH + B_VAL_WIDTH + 2 * GATE_RANK + MIX_WIDTH
ODD_IN_WIDTH = 4 * MIX_WIDTH

kernel_name = "hybrid_hgrn2_gla_diffattn_dit"


def rms_norm(x, gain):
    xf = x.astype(jnp.float32)
    y = xf * lax.rsqrt(jnp.mean(xf * xf, axis=-1, keepdims=True) + EPS)
    return (y * gain.astype(jnp.float32)).astype(x.dtype)


def adaln(cond, w, b):
    m = jax.nn.silu(cond) @ w + b
    return jnp.split(m, 3, axis=-1)


def to_heads(z, n_heads):
    bsz, length, _ = z.shape
    return z.reshape(bsz, length, n_heads, -1).transpose(0, 2, 1, 3)


def from_heads(z):
    bsz, n_heads, length, d = z.shape
    return z.transpose(0, 2, 1, 3).reshape(bsz, length, n_heads * d)


def grid_positions(n_tokens):
    rows = n_tokens // GRID_W
    pos_r = jnp.repeat(jnp.arange(rows, dtype=jnp.int32), GRID_W)
    pos_c = jnp.tile(jnp.arange(GRID_W, dtype=jnp.int32), rows)
    return pos_r, pos_c


def rope_1d(x, pos):
    half = x.shape[-1] // 2
    inv = ROPE_BASE ** (-jnp.arange(half, dtype=jnp.float32) / half)
    ang = pos.astype(jnp.float32)[:, None] * inv
    cos, sin = jnp.cos(ang), jnp.sin(ang)
    x1, x2 = x[..., :half], x[..., half:]
    return jnp.concatenate([x1 * cos - x2 * sin, x2 * cos + x1 * sin], axis=-1)


def rope_2d(x, pos_r, pos_c):
    r = x.shape[-1] // 2
    return jnp.concatenate([rope_1d(x[..., :r], pos_r), rope_1d(x[..., r:], pos_c)], axis=-1)


def gated_linear_scan(q, k, v, log_a, s0):
    bsz, n_heads, length, dk = q.shape
    n_chunks = length // CHUNK

    def to_chunks(z):
        z = z.astype(jnp.float32).reshape(bsz, n_heads, n_chunks, CHUNK, z.shape[-1])
        return jnp.moveaxis(z, 2, 0)

    mask = jnp.tril(jnp.ones((CHUNK, CHUNK), dtype=bool))

    def step(s, xs):
        qn, kn, vn, an = xs
        b = jnp.cumsum(an, axis=-2)
        q_dec = qn * jnp.exp(b)
        k_inv = kn * jnp.exp(-b)
        scores = jnp.where(mask, jnp.einsum("bhtd,bhsd->bhts", q_dec, k_inv), 0.0)
        o = jnp.einsum("bhts,bhsv->bhtv", scores, vn) + jnp.einsum("bhtd,bhdv->bhtv", q_dec, s)
        b_last = b[..., -1:, :]
        k_tail = kn * jnp.exp(b_last - b)
        s_new = jnp.exp(b_last)[..., 0, :, None] * s + jnp.einsum("bhsd,bhsv->bhdv", k_tail, vn)
        return s_new, o

    s_fin, o_chunks = lax.scan(step, s0.astype(jnp.float32),
                               (to_chunks(q), to_chunks(k), to_chunks(v), to_chunks(log_a)))
    o = jnp.moveaxis(o_chunks, 0, 2).reshape(bsz, n_heads, length, v.shape[-1])
    return o, s_fin


def bidir_prefix_scan(q, k_f, k_b, v, la_f, la_b, n_ctx):
    bsz, n_heads, _, dk = q.shape
    s0 = jnp.zeros((bsz, n_heads, dk, v.shape[-1]), jnp.float32)

    def split(z):
        return z[:, :, :n_ctx], z[:, :, n_ctx:]

    def flip(z):
        return jnp.flip(z, axis=2)

    qc, ql = split(q)
    kfc, kfl = split(k_f)
    kbc, kbl = split(k_b)
    vc, vl = split(v)
    afc, afl = split(la_f)
    abc, abl = split(la_b)
    o_cf, s_f = gated_linear_scan(qc, kfc, vc, afc, s0)
    o_lf, _ = gated_linear_scan(ql, kfl, vl, afl, s_f)
    o_cb, s_b = gated_linear_scan(flip(qc), flip(kbc), flip(vc), flip(abc), s0)
    o_lb, _ = gated_linear_scan(flip(ql), flip(kbl), flip(vl), flip(abl), s_b)
    return jnp.concatenate([o_cf + flip(o_cb), o_lf + flip(o_lb)], axis=2)


def even_mixer(h, n_ctx, layer, lb_logits, w_in, w_a2, b_a2, a_gain, b_gain, w_out):
    sizes = [A_WIDTH] * 4 + [B_KEY_WIDTH, B_KEY_WIDTH, B_VAL_WIDTH, GATE_RANK, GATE_RANK]
    cuts = np.cumsum(sizes).tolist()
    a_q, a_ff, a_fb, a_i, b_q, b_k, b_v, b_zf, b_zb, gate = jnp.split(h @ w_in, cuts, axis=-1)
    lb = jnp.cumsum(jax.nn.softmax(lb_logits.astype(jnp.float32), axis=0), axis=0)[layer]
    f_f = lb + (1.0 - lb) * jax.nn.sigmoid(a_ff.astype(jnp.float32))
    f_b = lb + (1.0 - lb) * jax.nn.sigmoid(a_fb.astype(jnp.float32))
    o_a = bidir_prefix_scan(
        to_heads(jax.nn.silu(a_q), A_HEADS),
        to_heads(1.0 - f_f, A_HEADS), to_heads(1.0 - f_b, A_HEADS),
        to_heads(a_i, A_HEADS),
        to_heads(jnp.log(f_f), A_HEADS), to_heads(jnp.log(f_b), A_HEADS), n_ctx)
    o_a = from_heads(rms_norm(o_a, a_gain))
    la_f = jax.nn.log_sigmoid((b_zf @ w_a2[0] + b_a2[0]).astype(jnp.float32)) / GLA_GATE_TEMP
    la_b = jax.nn.log_sigmoid((b_zb @ w_a2[1] + b_a2[1]).astype(jnp.float32)) / GLA_GATE_TEMP
    k_heads = to_heads(b_k, B_HEADS)
    o_b = bidir_prefix_scan(
        to_heads(b_q * (B_KEY_DIM ** -0.5), B_HEADS), k_heads, k_heads,
        to_heads(b_v, B_HEADS),
        to_heads(la_f, B_HEADS), to_heads(la_b, B_HEADS), n_ctx)
    o_b = from_heads(rms_norm(o_b, b_gain))
    o = jnp.concatenate([o_a, o_b], axis=-1).astype(h.dtype) * jax.nn.silu(gate)
    return o @ w_out


def odd_mixer(h, n_ctx, layer, w_in, q_gain, k_gain, lam_qk, o_gain, w_out, with_ctx_out):
    bsz, length, _ = h.shape
    n_lat = length - n_ctx
    q, k, v, gate = jnp.split(h @ w_in, 4, axis=-1)

    def pair_heads(z):
        return z.reshape(bsz, length, C_HEADS, 2, C_HEAD_DIM).transpose(3, 0, 2, 1, 4)

    q = rms_norm(pair_heads(q), q_gain)
    k = rms_norm(pair_heads(k), k_gain)
    v = to_heads(v, C_HEADS)
    pos_r, pos_c = grid_positions(n_lat)
    q_lat = rope_2d(q[..., n_ctx:, :], pos_r, pos_c)
    k_all = jnp.concatenate([k[..., :n_ctx, :].astype(jnp.float32),
                             rope_2d(k[..., n_ctx:, :], pos_r, pos_c)], axis=-2)
    lam_init = 0.8 - 0.6 * math.exp(-0.3 * layer)
    lq = lam_qk.astype(jnp.float32)
    lam = jnp.exp(jnp.sum(lq[0] * lq[1])) - jnp.exp(jnp.sum(lq[2] * lq[3])) + lam_init
    scale = C_HEAD_DIM ** -0.5

    def diff_attend(qq, kk, vv):
        s = jnp.einsum("nbhqd,nbhkd->nbhqk", qq.astype(jnp.float32), kk.astype(jnp.float32)) * scale
        p = jax.nn.softmax(s, axis=-1)
        return jnp.einsum("bhqk,bhkv->bhqv", p[0] - lam * p[1], vv.astype(jnp.float32))

    n_blk = n_lat // Q_BLOCK
    q_blocks = jnp.moveaxis(q_lat.reshape(2, bsz, C_HEADS, n_blk, Q_BLOCK, C_HEAD_DIM), 3, 0)
    o_lat = lax.map(lambda qb: diff_attend(qb, k_all, v), q_blocks)
    o = jnp.moveaxis(o_lat, 0, 2).reshape(bsz, C_HEADS, n_lat, C_VAL_DIM)
    if with_ctx_out:
        o_ctx = diff_attend(q[..., :n_ctx, :], k[..., :n_ctx, :], v[:, :, :n_ctx])
        o = jnp.concatenate([o_ctx, o], axis=2)
    else:
        gate = gate[:, n_ctx:]
    o = rms_norm(o, o_gain) * (1.0 - lam_init)
    return (from_heads(o).astype(h.dtype) * jax.nn.silu(gate)) @ w_out


def setup_inputs(seed: int = 0) -> dict:
    key = jax.random.key(seed)
    ks = jax.random.split(key, 20)
    f32 = jnp.float32
    D = D_MODEL

    def nrm(k, shape, scale):
        return jax.random.normal(k, shape, f32) * scale

    return {
        "x": nrm(ks[0], (BATCH, SEQ, D), 1.0),
        "c": nrm(ks[1], (BATCH, D), 1.0),
        "ctx": nrm(ks[2], (BATCH, CTX_LEN, D), 1.0),
        "c_ctx": nrm(ks[3], (D,), 1.0),
        "norm_gain": 1.0 + nrm(ks[4], (DEPTH, D), 0.02),
        "w_ada": nrm(ks[5], (DEPTH, D, 3 * D), 0.5 * D ** -0.5),
        "b_ada": nrm(ks[6], (DEPTH, 3 * D), 0.02),
        "lb_logits": nrm(ks[7], (DEPTH + 1, A_WIDTH), 0.1),
        "w_in_even": nrm(ks[8], (N_EVEN, D, EVEN_IN_WIDTH), D ** -0.5),
        "w_a2": nrm(ks[9], (N_EVEN, 2, GATE_RANK, B_KEY_WIDTH), GATE_RANK ** -0.5),
        "b_a2": nrm(ks[10], (N_EVEN, 2, B_KEY_WIDTH), 0.1),
        "a_out_gain": 1.0 + nrm(ks[11], (N_EVEN, A_HEAD_DIM), 0.02),
        "b_out_gain": 1.0 + nrm(ks[12], (N_EVEN, B_VAL_DIM), 0.02),
        "w_out_even": nrm(ks[13], (N_EVEN, MIX_WIDTH, D), MIX_WIDTH ** -0.5),
        "w_in_odd": nrm(ks[14], (N_ODD, D, ODD_IN_WIDTH), D ** -0.5),
        "q_norm_gain": 1.0 + nrm(ks[15], (N_ODD, C_HEAD_DIM), 0.02),
        "k_norm_gain": 1.0 + nrm(ks[16], (N_ODD, C_HEAD_DIM), 0.02),
        "lambda_qk": nrm(ks[17], (N_ODD, 4, C_HEAD_DIM), 0.1),
        "c_out_gain": 1.0 + nrm(ks[18], (N_ODD, C_VAL_DIM), 0.02),
        "w_out_odd": nrm(ks[19], (N_ODD, MIX_WIDTH, D), MIX_WIDTH ** -0.5),
    }


def reference(x, c, ctx, c_ctx, norm_gain, w_ada, b_ada, lb_logits, w_in_even, w_a2, b_a2,
              a_out_gain, b_out_gain, w_out_even, w_in_odd, q_norm_gain, k_norm_gain,
              lambda_qk, c_out_gain, w_out_odd):
    n_ctx = ctx.shape[1]
    n_lat = x.shape[1]
    for l in range(DEPTH):
        last = l == DEPTH - 1
        j = l // 2
        shift, scale, gate = adaln(c, w_ada[l], b_ada[l])
        shift_c, scale_c, gate_c = adaln(c_ctx, w_ada[l], b_ada[l])
        h_lat = rms_norm(x, norm_gain[l]) * (1.0 + scale[:, None]) + shift[:, None]
        h_ctx = rms_norm(ctx, norm_gain[l]) * (1.0 + scale_c) + shift_c
        h = jnp.concatenate([h_ctx, h_lat.astype(h_ctx.dtype)], axis=1)
        if l % 2 == 0:
            out = even_mixer(h, n_ctx, l, lb_logits, w_in_even[j], w_a2[j], b_a2[j],
                             a_out_gain[j], b_out_gain[j], w_out_even[j])
        else:
            out = odd_mixer(h, n_ctx, l, w_in_odd[j], q_norm_gain[j], k_norm_gain[j],
                            lambda_qk[j], c_out_gain[j], w_out_odd[j], not last)
        x = x + gate[:, None] * out[:, -n_lat:]
        if not last:
            ctx = ctx + gate_c * out[:, :n_ctx]
    return x
```

```python
import functools
import math

import jax
import jax.numpy as jnp
from jax import lax
from jax.experimental import pallas as pl
from jax.experimental.pallas import tpu as pltpu

F32 = jnp.float32
BF16 = jnp.bfloat16

EPS = 1e-6
GRID_W = 64
ROPE_BASE = 10000.0
CHUNK = 32
BLOCK = 256
CPB = BLOCK // CHUNK
A_HEAD_DIM = 128
B_KEY_DIM = 128
B_VAL_DIM = 256
GATE_RANK = 16
GLA_GATE_TEMP = 16.0
C_HEAD_DIM = 128
C_VAL_DIM = 256
LANES = 128
VMEM_LIMIT = 56 * 1024 * 1024


def _sigmoid(x):
    return 1.0 / (1.0 + jnp.exp(-x))


def _silu(x):
    return x * _sigmoid(x)


def _rms(x, gain):
    return x * lax.rsqrt(jnp.mean(x * x, axis=-1, keepdims=True) + EPS) * gain


def _nt_dot(a, b):
    return lax.dot_general(a, b, (((1,), (1,)), ((), ())), preferred_element_type=F32)


def _tn_dot(a, b):
    return lax.dot_general(a, b, (((0,), (0,)), ((), ())), preferred_element_type=F32)


def _adaln_kernel(cc_ref, w_ref, b_ref, o_ref):
    a = _silu(cc_ref[...]).astype(BF16)
    w = w_ref[0].astype(BF16)
    o_ref[0] = jnp.dot(a, w, preferred_element_type=F32) + b_ref[0]


def _adaln(cc, w_ada, b_ada, tn=512):
    depth, d, n = w_ada.shape
    rows = cc.shape[0]
    return pl.pallas_call(
        _adaln_kernel,
        grid=(depth, n // tn),
        in_specs=[
            pl.BlockSpec((rows, d), lambda l, j: (0, 0)),
            pl.BlockSpec((1, d, tn), lambda l, j: (l, 0, j)),
            pl.BlockSpec((1, 1, tn), lambda l, j: (l, 0, j)),
        ],
        out_specs=pl.BlockSpec((1, rows, tn), lambda l, j: (l, 0, j)),
        out_shape=jax.ShapeDtypeStruct((depth, rows, n), F32),
        compiler_params=pltpu.CompilerParams(
            dimension_semantics=("arbitrary", "arbitrary"), vmem_limit_bytes=VMEM_LIMIT),
        name="adaln",
    )(cc, w_ada, b_ada.reshape(depth, 1, n))


def _inproj_kernel(*refs, tm, n_lat, with_z):
    if with_z:
        x_ref, mb_ref, mc_ref, g_ref, w_ref, wz_ref, o_ref, z_ref, h_ref = refs
    else:
        x_ref, mb_ref, mc_ref, g_ref, w_ref, o_ref, h_ref = refs
    t = pl.program_id(1)
    j = pl.program_id(2)

    @pl.when(j == 0)
    def _():
        x = x_ref[0]
        y = _rms(x, g_ref[...])
        row = t * tm + lax.broadcasted_iota(jnp.int32, (tm, 1), 0)
        is_ctx = row >= n_lat
        shift = jnp.where(is_ctx, mc_ref[0, 0:1, :], mb_ref[0, 0:1, :])
        scale = jnp.where(is_ctx, mc_ref[0, 1:2, :], mb_ref[0, 1:2, :])
        h = (y * (1.0 + scale) + shift).astype(BF16)
        h_ref[...] = h
        if with_z:
            z_ref[0] = jnp.dot(h, wz_ref[...], preferred_element_type=F32).astype(z_ref.dtype)

    o_ref[0] = jnp.dot(h_ref[...], w_ref[...], preferred_element_type=F32).astype(o_ref.dtype)


def _inproj(xc, mod, gain, w, wz, *, n_lat, tm=768, tn=512):
    bsz, length, d = xc.shape
    n = w.shape[1]
    ctx_row = bsz
    with_z = wz is not None
    in_specs = [
        pl.BlockSpec((1, tm, d), lambda b, t, j: (b, t, 0)),
        pl.BlockSpec((1, 3, d), lambda b, t, j: (b, 0, 0)),
        pl.BlockSpec((1, 3, d), lambda b, t, j: (ctx_row, 0, 0)),
        pl.BlockSpec((1, d), lambda b, t, j: (0, 0)),
        pl.BlockSpec((d, tn), lambda b, t, j: (0, j)),
    ]
    args = [xc, mod, mod, gain.reshape(1, d), w]
    out_specs = [pl.BlockSpec((1, tm, tn), lambda b, t, j: (b, t, j))]
    out_shape = [jax.ShapeDtypeStruct((bsz, length, n), BF16)]
    if with_z:
        nz = wz.shape[1]
        in_specs.append(pl.BlockSpec((d, nz), lambda b, t, j: (0, 0)))
        args.append(wz)
        out_specs.append(pl.BlockSpec((1, tm, nz), lambda b, t, j: (b, t, 0)))
        out_shape.append(jax.ShapeDtypeStruct((bsz, length, nz), BF16))
    return pl.pallas_call(
        functools.partial(_inproj_kernel, tm=tm, n_lat=n_lat, with_z=with_z),
        grid=(bsz, length // tm, n // tn),
        in_specs=in_specs,
        out_specs=out_specs,
        out_shape=out_shape,
        scratch_shapes=[pltpu.VMEM((tm, d), BF16)],
        compiler_params=pltpu.CompilerParams(
            dimension_semantics=("parallel", "parallel", "arbitrary"),
            vmem_limit_bytes=VMEM_LIMIT),
        name="inproj_z" if with_z else "inproj",
    )(*args)


def _outproj_kernel(*refs, tm, n_lat, n_o):
    o_refs = refs[:n_o]
    gate_ref, mb_ref, mc_ref, w_ref, x_ref, out_ref, y_ref = refs[n_o:]
    t = pl.program_id(1)
    j = pl.program_id(2)

    @pl.when(j == 0)
    def _():
        col = 0
        for o_ref in o_refs:
            width = o_ref.shape[-1]
            g = gate_ref[0, :, col:col + width].astype(F32)
            y_ref[:, col:col + width] = (o_ref[0].astype(F32) * _silu(g)).astype(BF16)
            col += width

    acc = jnp.dot(y_ref[...], w_ref[...], preferred_element_type=F32)
    row = t * tm + lax.broadcasted_iota(jnp.int32, (tm, 1), 0)
    gvec = jnp.where(row >= n_lat, mc_ref[0, 2:3, :], mb_ref[0, 2:3, :])
    out_ref[0] = x_ref[0] + gvec * acc


def _outproj(o_list, p, gate_blk, mod, w_out, xc, *, rows, n_lat, tm, tn=512):
    bsz = xc.shape[0]
    d = w_out.shape[1]
    mix = w_out.shape[0]
    ctx_row = bsz
    in_specs = [pl.BlockSpec((1, tm, o.shape[-1]), lambda b, t, j: (b, t, 0)) for o in o_list]
    in_specs += [
        pl.BlockSpec((1, tm, mix), lambda b, t, j: (b, t, gate_blk)),
        pl.BlockSpec((1, 3, tn), lambda b, t, j: (b, 0, j)),
        pl.BlockSpec((1, 3, tn), lambda b, t, j: (ctx_row, 0, j)),
        pl.BlockSpec((mix, tn), lambda b, t, j: (0, j)),
        pl.BlockSpec((1, tm, tn), lambda b, t, j: (b, t, j)),
    ]
    return pl.pallas_call(
        functools.partial(_outproj_kernel, tm=tm, n_lat=n_lat, n_o=len(o_list)),
        grid=(bsz, rows // tm, d // tn),
        in_specs=in_specs,
        out_specs=pl.BlockSpec((1, tm, tn), lambda b, t, j: (b, t, j)),
        out_shape=jax.ShapeDtypeStruct((bsz, rows, d), F32),
        scratch_shapes=[pltpu.VMEM((tm, mix), BF16)],
        compiler_params=pltpu.CompilerParams(
            dimension_semantics=("parallel", "parallel", "arbitrary"),
            vmem_limit_bytes=VMEM_LIMIT),
        name="outproj",
    )(*o_list, p, mod, mod, w_out, xc)


def _chunk_cumsum(src_ref, dst_ref, r0):
    acc = jnp.zeros((CPB, src_ref.shape[-1]), F32)
    for r in range(CHUNK):
        acc = acc + src_ref[pl.ds(r0 + r, CPB, stride=CHUNK), :]
        dst_ref[pl.ds(r0 + r, CPB, stride=CHUNK), :] = acc
    return acc


def _rows_from_chunks(t):
    return jnp.concatenate(
        [jnp.broadcast_to(t[c:c + 1, :], (CHUNK, t.shape[-1])) for c in range(CPB)], axis=0)


def _prep_direction(q, k, la, la_ref, p_ref, qd_ref, ki_ref, kt_ref, dl_ref, r0, c0, backward):
    la_ref[pl.ds(r0, BLOCK), :] = la
    tot = _chunk_cumsum(la_ref, p_ref, r0)
    pref = p_ref[pl.ds(r0, BLOCK), :]
    tot_rows = _rows_from_chunks(tot)
    b = (tot_rows - pref + la) if backward else pref
    qd_ref[pl.ds(r0, BLOCK), :] = (q * jnp.exp(b)).astype(BF16)
    ki_ref[pl.ds(r0, BLOCK), :] = (k * jnp.exp(-b)).astype(BF16)
    kt_ref[pl.ds(r0, BLOCK), :] = (k * jnp.exp(tot_rows - b)).astype(BF16)
    dl_ref[pl.ds(c0, CPB), :] = jnp.exp(tot)


def _scan_block(qd_ref, ki_ref, kt_ref, dl_ref, v, st_ref, o_ref, blk, mask, backward):
    r0 = pl.multiple_of(blk * BLOCK, BLOCK)
    c0 = pl.multiple_of(blk * CPB, CPB)
    qd = qd_ref[pl.ds(r0, BLOCK), :]
    ki = ki_ref[pl.ds(r0, BLOCK), :]
    kt = kt_ref[pl.ds(r0, BLOCK), :]
    dl = dl_ref[pl.ds(c0, CPB), :]
    s = jnp.where(mask, _nt_dot(qd, ki), 0.0).astype(BF16)
    o = jnp.dot(s, v, preferred_element_type=F32)
    st = st_ref[...]
    pieces = [None] * CPB
    for c in (reversed(range(CPB)) if backward else range(CPB)):
        sl = slice(c * CHUNK, (c + 1) * CHUNK)
        pieces[c] = _nt_dot(qd[sl], st.astype(BF16))
        st = st * dl[c:c + 1, :] + _tn_dot(v[sl], kt[sl])
    st_ref[...] = st
    o_ref[pl.ds(r0, BLOCK), :] = o + jnp.concatenate(pieces, axis=0)


def _scan_all(v_of, scr, n_lat_blocks):
    (qdf, kif, ktf, dlf, qdb, kib, ktb, dlb, stf, stb, of, ob) = scr
    row = lax.broadcasted_iota(jnp.int32, (BLOCK, BLOCK), 0)
    col = lax.broadcasted_iota(jnp.int32, (BLOCK, BLOCK), 1)
    same = (row // CHUNK) == (col // CHUNK)
    mask_f = same & (col <= row)
    mask_b = same & (col >= row)
    stf[...] = jnp.zeros_like(stf)
    stb[...] = jnp.zeros_like(stb)
    n = n_lat_blocks

    def step(i, carry):
        blk_f = jnp.where(i == 0, n, i - 1)
        blk_b = jnp.where(i == 0, n, n - i)
        _scan_block(qdf, kif, ktf, dlf, v_of(blk_f), stf, of, blk_f, mask_f, False)
        _scan_block(qdb, kib, ktb, dlb, v_of(blk_b), stb, ob, blk_b, mask_b, True)
        return carry

    lax.fori_loop(0, n + 1, step, 0)


def _scan_scratch(length, dk, dv):
    nchunks = length // CHUNK
    per_dir = [pltpu.VMEM((length, dk), BF16)] * 3 + [pltpu.VMEM((nchunks, dk), F32)]
    return (per_dir + per_dir
            + [pltpu.VMEM((dv, dk), F32)] * 2
            + [pltpu.VMEM((length, dv), F32)] * 2
            + [pltpu.VMEM((length, dk), F32)] * 2)


def _hgrn_kernel(aq_ref, ff_ref, fb_ref, ai_ref, lbl_ref, gain_ref, o_ref, *scr, layer, n_blocks):
    scan_scr, (la_ref, p_ref) = scr[:12], scr[12:]
    (qdf, kif, ktf, dlf, qdb, kib, ktb, dlb, _, _, of, ob) = scan_scr
    lg = lbl_ref[...]
    e = jnp.exp(lg - jnp.max(lg, axis=0, keepdims=True))
    lb = jnp.sum(e[:layer + 1], axis=0, keepdims=True) / jnp.sum(e, axis=0, keepdims=True)

    def prep(i, carry):
        r0 = pl.multiple_of(i * BLOCK, BLOCK)
        c0 = pl.multiple_of(i * CPB, CPB)
        q = _silu(aq_ref[0, pl.ds(r0, BLOCK), :].astype(F32))
        f_f = lb + (1.0 - lb) * _sigmoid(ff_ref[0, pl.ds(r0, BLOCK), :].astype(F32))
        f_b = lb + (1.0 - lb) * _sigmoid(fb_ref[0, pl.ds(r0, BLOCK), :].astype(F32))
        _prep_direction(q, 1.0 - f_f, jnp.log(f_f), la_ref, p_ref, qdf, kif, ktf, dlf, r0, c0, False)
        _prep_direction(q, 1.0 - f_b, jnp.log(f_b), la_ref, p_ref, qdb, kib, ktb, dlb, r0, c0, True)
        return carry

    lax.fori_loop(0, n_blocks, prep, 0)

    def v_of(blk):
        return ai_ref[0, pl.ds(pl.multiple_of(blk * BLOCK, BLOCK), BLOCK), :]

    _scan_all(v_of, scan_scr, n_blocks - 1)

    def fin(i, carry):
        r0 = pl.multiple_of(i * BLOCK, BLOCK)
        o = of[pl.ds(r0, BLOCK), :] + ob[pl.ds(r0, BLOCK), :]
        o_ref[0, pl.ds(r0, BLOCK), :] = _rms(o, gain_ref[...]).astype(o_ref.dtype)
        return carry

    lax.fori_loop(0, n_blocks, fin, 0)


def _hgrn(p, lb_logits, gain, *, layer, heads):
    bsz, length, _ = p.shape
    dk = A_HEAD_DIM

    def col(k):
        return pl.BlockSpec((1, length, dk), lambda b, h: (b, 0, k * heads + h))

    return pl.pallas_call(
        functools.partial(_hgrn_kernel, layer=layer, n_blocks=length // BLOCK),
        grid=(bsz, heads),
        in_specs=[col(0), col(1), col(2), col(3),
                  pl.BlockSpec((lb_logits.shape[0], dk), lambda b, h: (0, h)),
                  pl.BlockSpec((1, dk), lambda b, h: (0, 0))],
        out_specs=pl.BlockSpec((1, length, dk), lambda b, h: (b, 0, h)),
        out_shape=jax.ShapeDtypeStruct((bsz, length, heads * dk), BF16),
        scratch_shapes=_scan_scratch(length, dk, dk),
        compiler_params=pltpu.CompilerParams(
            dimension_semantics=("parallel", "parallel"), vmem_limit_bytes=VMEM_LIMIT),
        name="hgrn2_scan",
    )(p, p, p, p, lb_logits, gain.reshape(1, dk))


def _gla_kernel(q_ref, k_ref, v_ref, z_ref, w2_ref, b2_ref, gain_ref, o_ref, *scr, n_blocks):
    scan_scr, (la_ref, p_ref) = scr[:12], scr[12:]
    (qdf, kif, ktf, dlf, qdb, kib, ktb, dlb, _, _, of, ob) = scan_scr

    def log_decay(z, d):
        x = jnp.dot(z, w2_ref[d], preferred_element_type=F32) + b2_ref[d]
        return (jnp.minimum(x, 0.0) - jnp.log(1.0 + jnp.exp(-jnp.abs(x)))) / GLA_GATE_TEMP

    def prep(i, carry):
        r0 = pl.multiple_of(i * BLOCK, BLOCK)
        c0 = pl.multiple_of(i * CPB, CPB)
        q = q_ref[0, pl.ds(r0, BLOCK), :].astype(F32) * (B_KEY_DIM ** -0.5)
        k = k_ref[0, pl.ds(r0, BLOCK), :].astype(F32)
        z = z_ref[0, pl.ds(r0, BLOCK), :]
        _prep_direction(q, k, log_decay(z, 0), la_ref, p_ref, qdf, kif, ktf, dlf, r0, c0, False)
        _prep_direction(q, k, log_decay(z, 1), la_ref, p_ref, qdb, kib, ktb, dlb, r0, c0, True)
        return carry

    lax.fori_loop(0, n_blocks, prep, 0)

    def v_of(blk):
        return v_ref[0, pl.ds(pl.multiple_of(blk * BLOCK, BLOCK), BLOCK), :]

    _scan_all(v_of, scan_scr, n_blocks - 1)

    def fin(i, carry):
        r0 = pl.multiple_of(i * BLOCK, BLOCK)
        o = of[pl.ds(r0, BLOCK), :] + ob[pl.ds(r0, BLOCK), :]
        o_ref[0, pl.ds(r0, BLOCK), :] = _rms(o, gain_ref[...]).astype(o_ref.dtype)
        return carry

    lax.fori_loop(0, n_blocks, fin, 0)


def _gla(p, z, w2p, b2, gain, *, heads, q_blk0, k_blk0, v_blk0):
    bsz, length, _ = p.shape
    dk, dv = B_KEY_DIM, B_VAL_DIM
    nz = z.shape[-1]
    return pl.pallas_call(
        functools.partial(_gla_kernel, n_blocks=length // BLOCK),
        grid=(bsz, heads),
        in_specs=[
            pl.BlockSpec((1, length, dk), lambda b, h: (b, 0, q_blk0 + h)),
            pl.BlockSpec((1, length, dk), lambda b, h: (b, 0, k_blk0 + h)),
            pl.BlockSpec((1, length, dv), lambda b, h: (b, 0, v_blk0 + h)),
            pl.BlockSpec((1, length, nz), lambda b, h: (b, 0, 0)),
            pl.BlockSpec((2, nz, dk), lambda b, h: (0, 0, h)),
            pl.BlockSpec((2, 1, dk), lambda b, h: (0, 0, h)),
            pl.BlockSpec((1, dv), lambda b, h: (0, 0)),
        ],
        out_specs=pl.BlockSpec((1, length, dv), lambda b, h: (b, 0, h)),
        out_shape=jax.ShapeDtypeStruct((bsz, length, heads * dv), BF16),
        scratch_shapes=_scan_scratch(length, dk, dv),
        compiler_params=pltpu.CompilerParams(
            dimension_semantics=("parallel", "parallel"), vmem_limit_bytes=VMEM_LIMIT),
        name="gla_scan",
    )(p, p, p, z, w2p, b2, gain.reshape(1, dv))


def _rope(y, cos, sin):
    quarter = y.shape[-1] // 4
    lane = lax.broadcasted_iota(jnp.int32, y.shape, 1)
    first = (lane // quarter) % 2 == 0
    partner = jnp.where(first, pltpu.roll(y, y.shape[-1] - quarter, 1), pltpu.roll(y, quarter, 1))
    return y * cos + partner * sin


def _attn_kernel(q_ref, k_ref, v_ref, cq_ref, sq_ref, ck_ref, sk_ref, qg_ref, kg_ref, lam_ref,
                 og_ref, o_ref, kn_ref, *, n_lat, lam_init):
    hd = C_HEAD_DIM

    @pl.when(pl.program_id(2) == 0)
    def _():
        for p in range(2):
            cs = slice(p * hd, (p + 1) * hd)
            k_lat = _rms(k_ref[0, :n_lat, cs].astype(F32), kg_ref[...])
            kn_ref[:n_lat, cs] = _rope(k_lat, ck_ref[...], sk_ref[...]).astype(BF16)
            kn_ref[n_lat:, cs] = _rms(k_ref[0, n_lat:, cs].astype(F32), kg_ref[...]).astype(BF16)

    lq = lam_ref[...]
    lam = (jnp.exp(jnp.sum(lq[0:1] * lq[1:2], axis=-1, keepdims=True))
           - jnp.exp(jnp.sum(lq[2:3] * lq[3:4], axis=-1, keepdims=True)) + lam_init)
    probs = []
    for p in range(2):
        cs = slice(p * hd, (p + 1) * hd)
        q = _rope(_rms(q_ref[0, :, cs].astype(F32), qg_ref[...]), cq_ref[...], sq_ref[...])
        s = _nt_dot((q * (hd ** -0.5)).astype(BF16), kn_ref[:, cs])
        e = jnp.exp(s - jnp.max(s, axis=-1, keepdims=True))
        inv = 1.0 / jnp.sum(e, axis=-1, keepdims=True)
        probs.append((e, inv))
    (e0, i0), (e1, i1) = probs
    pm = (e0 * i0 - e1 * (lam * i1)).astype(BF16)
    o = jnp.dot(pm, v_ref[0], preferred_element_type=F32)
    o_ref[0] = (_rms(o, og_ref[...]) * (1.0 - lam_init)).astype(o_ref.dtype)


def _rope_tables(n_lat):
    quarter = C_HEAD_DIM // 4
    inv = ROPE_BASE ** (-jnp.arange(quarter, dtype=F32) / quarter)
    t = jnp.arange(n_lat, dtype=jnp.int32)
    ang_r = (t // GRID_W).astype(F32)[:, None] * inv
    ang_c = (t % GRID_W).astype(F32)[:, None] * inv
    cos = jnp.concatenate([jnp.cos(ang_r)] * 2 + [jnp.cos(ang_c)] * 2, axis=-1)
    sin = jnp.concatenate([-jnp.sin(ang_r), jnp.sin(ang_r), -jnp.sin(ang_c), jnp.sin(ang_c)], axis=-1)
    return cos, sin


def _attention(p, q_gain, k_gain, lam_qk, o_gain, *, n_lat, heads, layer, tq=256):
    bsz, length, _ = p.shape
    hd, dv = C_HEAD_DIM, C_VAL_DIM
    lam_init = 0.8 - 0.6 * math.exp(-0.3 * layer)
    cos, sin = _rope_tables(n_lat)
    return pl.pallas_call(
        functools.partial(_attn_kernel, n_lat=n_lat, lam_init=lam_init),
        grid=(bsz, heads, n_lat // tq),
        in_specs=[
            pl.BlockSpec((1, tq, 2 * hd), lambda b, h, i: (b, i, h)),
            pl.BlockSpec((1, length, 2 * hd), lambda b, h, i: (b, 0, heads + h)),
            pl.BlockSpec((1, length, dv), lambda b, h, i: (b, 0, 2 * heads + h)),
            pl.BlockSpec((tq, hd), lambda b, h, i: (i, 0)),
            pl.BlockSpec((tq, hd), lambda b, h, i: (i, 0)),
            pl.BlockSpec((n_lat, hd), lambda b, h, i: (0, 0)),
            pl.BlockSpec((n_lat, hd), lambda b, h, i: (0, 0)),
            pl.BlockSpec((1, hd), lambda b, h, i: (0, 0)),
            pl.BlockSpec((1, hd), lambda b, h, i: (0, 0)),
            pl.BlockSpec((4, hd), lambda b, h, i: (0, 0)),
            pl.BlockSpec((1, dv), lambda b, h, i: (0, 0)),
        ],
        out_specs=pl.BlockSpec((1, tq, dv), lambda b, h, i: (b, i, h)),
        out_shape=jax.ShapeDtypeStruct((bsz, n_lat, heads * dv), BF16),
        scratch_shapes=[pltpu.VMEM((length, 2 * hd), BF16)],
        compiler_params=pltpu.CompilerParams(
            dimension_semantics=("parallel", "parallel", "arbitrary"),
            vmem_limit_bytes=VMEM_LIMIT),
        name="diff_attn",
    )(p, p, p, cos, sin, cos, sin, q_gain.reshape(1, hd), k_gain.reshape(1, hd), lam_qk,
      o_gain.reshape(1, dv))


def kernel(x, c, ctx, c_ctx, norm_gain, w_ada, b_ada, lb_logits, w_in_even, w_a2, b_a2, a_out_gain,
           b_out_gain, w_out_even, w_in_odd, q_norm_gain, k_norm_gain, lambda_qk, c_out_gain,
           w_out_odd):
    bsz, n_lat, d = x.shape
    a_width = d // 2
    a_heads = a_width // A_HEAD_DIM
    b_key_width = d // 4
    b_heads = b_key_width // B_KEY_DIM
    c_heads = d // (2 * C_HEAD_DIM)

    pad = (-(bsz + 1)) % 16
    cc = jnp.concatenate([c, c_ctx[None, :], jnp.zeros((pad, d), F32)], axis=0)
    mod = _adaln(cc, w_ada, b_ada)
    mod = mod.reshape(mod.shape[0], mod.shape[1], 3, d)

    xc = jnp.concatenate([x, ctx], axis=1)

    w0 = w_in_even[0]
    z_lo = 4 * a_width + 2 * b_key_width + d // 2
    z_hi = z_lo + 2 * GATE_RANK
    w_main = jnp.concatenate([w0[:, :z_lo], w0[:, z_hi:]], axis=1).astype(BF16)
    w_z = jnp.pad(w0[:, z_lo:z_hi], ((0, 0), (0, LANES - 2 * GATE_RANK))).astype(BF16)
    p0, z0 = _inproj(xc, mod[0], norm_gain[0], w_main, w_z, n_lat=n_lat)
    o_a = _hgrn(p0, lb_logits, a_out_gain[0], layer=0, heads=a_heads)
    w2p = jnp.zeros((2, LANES, b_key_width), F32)
    w2p = w2p.at[0, :GATE_RANK].set(w_a2[0, 0]).at[1, GATE_RANK:2 * GATE_RANK].set(w_a2[0, 1])
    o_b = _gla(p0, z0, w2p.astype(BF16), b_a2[0].reshape(2, 1, b_key_width), b_out_gain[0],
               heads=b_heads, q_blk0=4 * a_width // B_KEY_DIM,
               k_blk0=(4 * a_width + b_key_width) // B_KEY_DIM,
               v_blk0=(4 * a_width + 2 * b_key_width) // B_VAL_DIM)
    xc = _outproj([o_a, o_b], p0, (z_lo // d), mod[0], w_out_even[0].astype(BF16), xc,
                  rows=xc.shape[1], n_lat=n_lat, tm=768)

    p1 = _inproj(xc, mod[1], norm_gain[1], w_in_odd[0].astype(BF16), None, n_lat=n_lat)[0]
    o_c = _attention(p1, q_norm_gain[0], k_norm_gain[0], lambda_qk[0], c_out_gain[0],
                     n_lat=n_lat, heads=c_heads, layer=1)
    return _outproj([o_c], p1, 3, mod[1], w_out_odd[0].astype(BF16), xc,
                    rows=n_lat, n_lat=n_lat, tm=512)
```
